```python
import math
import jax, jax.numpy as jnp
from jax import lax
import numpy as np

D_MODEL = 1024
BATCH = 2
SEQ = 8192
DEPTH = 4
DEC_BATCH = 128
DEC_SEQ = 1
PAST_LEN = 2048
PAGE_SIZE = 128

HEAD_DIM = 64
A_GROUPS = ((128, 1), (512, 4), (2048, 16))
N_GROUPS = len(A_GROUPS)
A_HEADS = 8
A_REACH = A_GROUPS[0][0] // A_GROUPS[0][1]
QB = 128
B_HEADS = 16
B_BIAS_NEAR = -5.0
B_BIAS_FAR = -9.0
FFN_HIDDEN = ((8 * D_MODEL + 3 * 256 - 1) // (3 * 256)) * 256
N_LAYERS_A = (DEPTH + 1) // 2
N_LAYERS_B = DEPTH // 2
EPS = 1e-6

kernel_name = "dilated_swa_stickbreak_hybrid_step"


def rms_norm(x, g):
    xf = x.astype(jnp.float32)
    y = xf * lax.rsqrt(jnp.mean(xf * xf, axis=-1, keepdims=True) + EPS)
    return (y * g.astype(jnp.float32)).astype(x.dtype)


def alibi_slopes(n_heads):
    start = 2.0 ** (-8.0 / n_heads)
    return jnp.array([start ** (i + 1) for i in range(n_heads)], dtype=jnp.float32)


def swiglu(h, w_gate, w_up, w_down):
    a = jnp.einsum('...d,df->...f', h, w_gate)
    b = jnp.einsum('...d,df->...f', h, w_up)
    return jnp.einsum('...f,fd->...d', jax.nn.silu(a) * b, w_down)


def a_project(h, w_qkv, q_gain, k_gain):
    qkv = jnp.einsum('...d,de->...e', h, w_qkv)
    qkv = qkv.reshape(*h.shape[:-1], N_GROUPS, 3, A_HEADS, HEAD_DIM)
    q = rms_norm(qkv[..., 0, :, :], q_gain[:, None, :])
    k = rms_norm(qkv[..., 1, :, :], k_gain[:, None, :])
    v = qkv[..., 2, :, :]
    return q, k, v


def dilated_prompt(q, k, v, dilation, slopes):
    b, s, h, e = q.shape
    n = s // dilation
    nblk = -(-n // QB)
    n_pad = nblk * QB

    def to_sub(x):
        return x.reshape(b, n, dilation, h, e).transpose(0, 2, 1, 3, 4)

    qb = jnp.pad(to_sub(q), ((0, 0), (0, 0), (0, n_pad - n), (0, 0), (0, 0)))
    qb = qb.reshape(b, dilation, nblk, QB, h, e)

    def band(x):
        xp = jnp.pad(to_sub(x), ((0, 0), (0, 0), (A_REACH, n_pad - n), (0, 0), (0, 0)))
        xp = xp.reshape(b, dilation, nblk + 1, QB, h, e)
        return jnp.concatenate([xp[:, :, :-1], xp[:, :, 1:]], axis=3)

    kb, vb = band(k), band(v)
    scores = jnp.einsum('bdnqhe,bdnkhe->bdnhqk', qb, kb,
                        preferred_element_type=jnp.float32) / math.sqrt(e)
    qi = jnp.arange(QB)[:, None]
    kk = jnp.arange(2 * QB)[None, :]
    steps = qi + A_REACH - kk
    key_u = (jnp.arange(nblk) * QB)[:, None, None] + kk[None] - A_REACH
    valid = (steps >= 0) & (steps <= A_REACH) & (key_u >= 0)
    bias = -slopes[:, None, None] * (dilation * steps).astype(jnp.float32)[None]
    scores = jnp.where(valid[None, None, :, None], scores + bias[None, None, None], -jnp.inf)
    lse = jax.nn.logsumexp(scores, axis=-1)
    p = jnp.exp(scores - lse[..., None]).astype(v.dtype)
    o = jnp.einsum('bdnhqk,bdnkhe->bdnqhe', p, vb)
    o = o.reshape(b, dilation, n_pad, h, e)[:, :, :n].transpose(0, 2, 1, 3, 4).reshape(b, s, h, e)
    lse = lse.transpose(0, 1, 2, 4, 3).reshape(b, dilation, n_pad, h)[:, :, :n]
    lse = lse.transpose(0, 2, 1, 3).reshape(b, s, h)
    return o, lse


def dilated_sample(q, k_new, v_new, buf, dilation, slopes):
    L = buf.shape[1]
    t = q.shape[1]
    e = q.shape[-1]
    kv_all = jnp.concatenate([buf.astype(k_new.dtype), jnp.stack([k_new, v_new], axis=2)], axis=1)
    steps = jnp.arange(A_REACH + 1)
    idx = L + jnp.arange(t)[:, None] - dilation * steps[None, :]
    valid = idx >= 0
    kv_sel = kv_all[:, jnp.maximum(idx, 0)]
    scores = jnp.einsum('bthe,btjhe->bthj', q, kv_sel[:, :, :, 0],
                        preferred_element_type=jnp.float32) / math.sqrt(e)
    bias = -slopes[:, None] * (dilation * steps).astype(jnp.float32)[None]
    scores = jnp.where(valid[None, :, None, :], scores + bias[None, None], -jnp.inf)
    lse = jax.nn.logsumexp(scores, axis=-1)
    p = jnp.exp(scores - lse[..., None]).astype(v_new.dtype)
    o = jnp.einsum('bthj,btjhe->bthe', p, kv_sel[:, :, :, 1])
    return o, lse, kv_all[:, -L:]


def combine_groups(outs, lses):
    o = jnp.stack(outs, axis=0)
    w = jax.nn.softmax(jnp.stack(lses, axis=0), axis=0)
    return jnp.sum(w[..., None] * o.astype(jnp.float32), axis=0).astype(outs[0].dtype)


def mixer_a_prompt(h, w_qkv, q_gain, k_gain, w_o):
    q, k, v = a_project(h, w_qkv, q_gain, k_gain)
    slopes = alibi_slopes(N_GROUPS * A_HEADS).reshape(N_GROUPS, A_HEADS)
    outs, lses, bufs = [], [], []
    for g, (window, dilation) in enumerate(A_GROUPS):
        o, lse = dilated_prompt(q[:, :, g], k[:, :, g], v[:, :, g], dilation, slopes[g])
        outs.append(o)
        lses.append(lse)
        keep = min(window, h.shape[1])
        bufs.append(jnp.stack([k[:, -keep:, g], v[:, -keep:, g]], axis=2))
    o = combine_groups(outs, lses)
    y = jnp.einsum('bshe,hed->bsd', o, w_o.reshape(A_HEADS, HEAD_DIM, D_MODEL))
    return y, bufs


def mixer_a_sample(h, caches, w_qkv, q_gain, k_gain, w_o):
    q, k, v = a_project(h, w_qkv, q_gain, k_gain)
    slopes = alibi_slopes(N_GROUPS * A_HEADS).reshape(N_GROUPS, A_HEADS)
    outs, lses, bufs = [], [], []
    for g, (window, dilation) in enumerate(A_GROUPS):
        o, lse, nb = dilated_sample(q[:, :, g], k[:, :, g], v[:, :, g], caches[g], dilation, slopes[g])
        outs.append(o)
        lses.append(lse)
        bufs.append(nb)
    o = combine_groups(outs, lses)
    y = jnp.einsum('bshe,hed->bsd', o, w_o.reshape(A_HEADS, HEAD_DIM, D_MODEL))
    return y, bufs


def stick_break_weights(z, mask):
    log_beta = jax.nn.log_sigmoid(z)
    log_keep = jnp.where(mask, jax.nn.log_sigmoid(-z), 0.0)
    after = lax.cumsum(log_keep, axis=z.ndim - 1, reverse=True) - log_keep
    return jnp.where(mask, jnp.exp(log_beta + after), 0.0)


def b_project(h, w_qkv):
    qkv = jnp.einsum('...d,de->...e', h, w_qkv).reshape(*h.shape[:-1], 3, B_HEADS, HEAD_DIM)
    return qkv[..., 0, :, :], qkv[..., 1, :, :], qkv[..., 2, :, :]


def mixer_b_prompt(h, w_qkv, logit_bias, w_o):
    q, k, v = b_project(h, w_qkv)
    b, s, hh, e = q.shape
    nblk = s // QB
    qb = q.reshape(b, nblk, QB, hh, e).transpose(1, 0, 2, 3, 4)
    key_pos = jnp.arange(s)
    bias = logit_bias.astype(jnp.float32)[None, :, None, None]

    def block(args):
        q_blk, blk = args
        z = jnp.einsum('bqhe,bkhe->bhqk', q_blk, k,
                       preferred_element_type=jnp.float32) / math.sqrt(e) + bias
        q_pos = blk * QB + jnp.arange(QB)
        w = stick_break_weights(z, key_pos[None, :] < q_pos[:, None])
        return jnp.einsum('bhqk,bkhe->bqhe', w.astype(v.dtype), v)

    o = lax.map(block, (qb, jnp.arange(nblk)))
    o = o.transpose(1, 0, 2, 3, 4).reshape(b, s, hh * e)
    y = jnp.einsum('bsf,fd->bsd', o, w_o)
    return y, jnp.stack([k, v], axis=2)


def mixer_b_sample(h, pages, page_table, w_qkv, logit_bias, w_o):
    q, k, v = b_project(h, w_qkv)
    bd, t, hh, e = q.shape
    past = pages[page_table].astype(k.dtype)
    past = past.reshape(bd, -1, 2, hh, e)
    plen = past.shape[1]
    k_all = jnp.concatenate([past[:, :, 0], k], axis=1)
    v_all = jnp.concatenate([past[:, :, 1], v], axis=1)
    z = jnp.einsum('bqhe,bkhe->bhqk', q, k_all, preferred_element_type=jnp.float32) / math.sqrt(e)
    z = z + logit_bias.astype(jnp.float32)[None, :, None, None]
    mask = jnp.arange(plen + t)[None, :] < (plen + jnp.arange(t))[:, None]
    w = stick_break_weights(z, mask)
    o = jnp.einsum('bhqk,bkhe->bqhe', w.astype(v.dtype), v_all).reshape(bd, t, hh * e)
    y = jnp.einsum('bsf,fd->bsd', o, w_o)
    return y, jnp.stack([k, v], axis=2)


def setup_inputs(seed: int = 0) -> dict:
    key = jax.random.key(seed)
    ks = jax.random.split(key, 24)
    f32 = jnp.float32
    n_pages = PAST_LEN // PAGE_SIZE
    n_pool = (DEC_BATCH * n_pages * 5 + 3) // 4
    a_cols = N_GROUPS * 3 * A_HEADS * HEAD_DIM
    b_cols = 3 * B_HEADS * HEAD_DIM

    def nrm(k, shape, scale=1.0):
        return jax.random.normal(k, shape, f32) * scale

    def win(k, window):
        return nrm(k, (DEC_BATCH, min(window, PAST_LEN), 2, A_HEADS, HEAD_DIM))

    def pool(k):
        return nrm(k, (n_pool, PAGE_SIZE, 2, B_HEADS, HEAD_DIM))

    page_table = jax.random.permutation(ks[10], n_pool)[: DEC_BATCH * n_pages]
    page_table = page_table.reshape(DEC_BATCH, n_pages).astype(jnp.int32)
    bias_base = jnp.linspace(B_BIAS_NEAR, B_BIAS_FAR, B_HEADS, dtype=f32)
    return {
        "x_prompt": nrm(ks[0], (BATCH, SEQ, D_MODEL)),
        "x_sample": nrm(ks[1], (DEC_BATCH, DEC_SEQ, D_MODEL)),
        "cache_l0_w128": win(ks[2], A_GROUPS[0][0]),
        "cache_l0_w512": win(ks[3], A_GROUPS[1][0]),
        "cache_l0_w2048": win(ks[4], A_GROUPS[2][0]),
        "cache_l1_kv_pages": pool(ks[5]),
        "cache_l2_w128": win(ks[6], A_GROUPS[0][0]),
        "cache_l2_w512": win(ks[7], A_GROUPS[1][0]),
        "cache_l2_w2048": win(ks[8], A_GROUPS[2][0]),
        "cache_l3_kv_pages": pool(ks[9]),
        "page_table": page_table,
        "norm_mix": 1.0 + nrm(ks[11], (DEPTH, D_MODEL), 0.05),
        "norm_ffn": 1.0 + nrm(ks[12], (DEPTH, D_MODEL), 0.05),
        "a_w_qkv": nrm(ks[13], (N_LAYERS_A, D_MODEL, a_cols), D_MODEL ** -0.5),
        "a_q_gain": 1.0 + nrm(ks[14], (N_LAYERS_A, N_GROUPS, HEAD_DIM), 0.05),
        "a_k_gain": 1.0 + nrm(ks[15], (N_LAYERS_A, N_GROUPS, HEAD_DIM), 0.05),
        "a_w_o": nrm(ks[16], (N_LAYERS_A, A_HEADS * HEAD_DIM, D_MODEL), (A_HEADS * HEAD_DIM) ** -0.5),
        "b_w_qkv": nrm(ks[17], (N_LAYERS_B, D_MODEL, b_cols), D_MODEL ** -0.5),
        "b_logit_bias": bias_base[None, :] + nrm(ks[22], (N_LAYERS_B, B_HEADS), 0.1),
        "b_w_o": nrm(ks[18], (N_LAYERS_B, B_HEADS * HEAD_DIM, D_MODEL), (B_HEADS * HEAD_DIM) ** -0.5),
        "ffn_w_gate": nrm(ks[19], (DEPTH, D_MODEL, FFN_HIDDEN), D_MODEL ** -0.5),
        "ffn_w_up": nrm(ks[20], (DEPTH, D_MODEL, FFN_HIDDEN), D_MODEL ** -0.5),
        "ffn_w_down": nrm(ks[21], (DEPTH, FFN_HIDDEN, D_MODEL), FFN_HIDDEN ** -0.5),
    }


def reference(x_prompt, x_sample, cache_l0_w128, cache_l0_w512, cache_l0_w2048, cache_l1_kv_pages,
              cache_l2_w128, cache_l2_w512, cache_l2_w2048, cache_l3_kv_pages, page_table,
              norm_mix, norm_ffn, a_w_qkv, a_q_gain, a_k_gain, a_w_o, b_w_qkv, b_logit_bias, b_w_o,
              ffn_w_gate, ffn_w_up, ffn_w_down):
    layer_caches = ((cache_l0_w128, cache_l0_w512, cache_l0_w2048), (cache_l1_kv_pages,),
                    (cache_l2_w128, cache_l2_w512, cache_l2_w2048), (cache_l3_kv_pages,))
    xp, xs = x_prompt, x_sample
    prompt_states, sample_states = [], []
    for i in range(DEPTH):
        j = i // 2
        hp = rms_norm(xp, norm_mix[i])
        hs = rms_norm(xs, norm_mix[i])
        if i % 2 == 0:
            yp, st_p = mixer_a_prompt(hp, a_w_qkv[j], a_q_gain[j], a_k_gain[j], a_w_o[j])
            ys, st_s = mixer_a_sample(hs, layer_caches[i], a_w_qkv[j], a_q_gain[j], a_k_gain[j], a_w_o[j])
            prompt_states.append(tuple(st_p))
            sample_states.append(tuple(st_s))
        else:
            yp, st_p = mixer_b_prompt(hp, b_w_qkv[j], b_logit_bias[j], b_w_o[j])
            ys, st_s = mixer_b_sample(hs, layer_caches[i][0], page_table, b_w_qkv[j], b_logit_bias[j], b_w_o[j])
            prompt_states.append((st_p,))
            sample_states.append((st_s,))
        xp = xp + yp
        xs = xs + ys
        xp = xp + swiglu(rms_norm(xp, norm_ffn[i]), ffn_w_gate[i], ffn_w_up[i], ffn_w_down[i])
        xs = xs + swiglu(rms_norm(xs, norm_ffn[i]), ffn_w_gate[i], ffn_w_up[i], ffn_w_down[i])
    y_prompt, y_sample = xp, xs
    p0_w128, p0_w512, p0_w2048 = prompt_states[0]
    s0_w128, s0_w512, s0_w2048 = sample_states[0]
    (p1_kv,) = prompt_states[1]
    (s1_kv,) = sample_states[1]
    p2_w128, p2_w512, p2_w2048 = prompt_states[2]
    s2_w128, s2_w512, s2_w2048 = sample_states[2]
    (p3_kv,) = prompt_states[3]
    (s3_kv,) = sample_states[3]
    return (y_prompt, y_sample, p0_w128, p0_w512, p0_w2048, s0_w128, s0_w512, s0_w2048, p1_kv, s1_kv,
            p2_w128, p2_w512, p2_w2048, s2_w128, s2_w512, s2_w2048, p3_kv, s3_kv)
```

```python
import functools
import math

import numpy as np
import jax
import jax.numpy as jnp
from jax import lax
from jax.experimental import pallas as pl
from jax.experimental.pallas import tpu as pltpu

F32 = jnp.float32
BF16 = jnp.bfloat16

HEAD_DIM = 64
A_HEADS = 8
A_WIDTH = A_HEADS * HEAD_DIM
B_HEADS = 16
B_WIDTH = B_HEADS * HEAD_DIM
A_GROUPS = ((128, 1), (512, 4), (2048, 16))
N_GROUPS = len(A_GROUPS)
A_REACH = 128
QB = 128
LANES = 128
EPS = 1e-6
NEG = -1e30
VMEM_LIMIT = 48 * 1024 * 1024


def _alibi_slopes():
    n = N_GROUPS * A_HEADS
    start = 2.0 ** (-8.0 / n)
    s = np.array([start ** (i + 1) for i in range(n)], dtype=np.float32)
    return s.reshape(N_GROUPS, A_HEADS)


_SLOPES = _alibi_slopes()


def _params(*sem):
    return pltpu.CompilerParams(dimension_semantics=sem, vmem_limit_bytes=VMEM_LIMIT)


def _rms_bf16(x, g):
    ms = jnp.mean(x * x, axis=-1, keepdims=True)
    return (x * lax.rsqrt(ms + EPS) * g).astype(BF16)


def _split_bf16(x):
    hi = x.astype(BF16)
    lo = (x - hi.astype(F32)).astype(BF16)
    return hi, lo


def _dot(a, b):
    return jnp.dot(a, b, preferred_element_type=F32)


def _dot_nt(a, b):
    return lax.dot_general(a, b, (((1,), (1,)), ((), ())), preferred_element_type=F32)


def _proj_a_kernel(x_ref, g_ref, w_ref, cg_ref, bd_ref, o_ref):
    h = _rms_bf16(x_ref[...], g_ref[...])
    for c in range(3 * N_GROUPS):
        cols = slice(c * A_WIDTH, (c + 1) * A_WIDTH)
        y = _dot(h, w_ref[:, cols])
        if c % 3 != 2:
            hi, lo = _split_bf16(y * y)
            ms = _dot(hi, bd_ref[...]) + _dot(lo, bd_ref[...])
            y = y * lax.rsqrt(ms + EPS) * cg_ref[:, cols]
        o_ref[:, cols] = y


def _proj_a(x, g, w, colgain, blockdiag, tm):
    n, d = x.shape
    e = w.shape[1]
    return pl.pallas_call(
        _proj_a_kernel,
        grid=(n // tm,),
        in_specs=[
            pl.BlockSpec((tm, d), lambda i: (i, 0)),
            pl.BlockSpec((1, d), lambda i: (0, 0)),
            pl.BlockSpec((d, e), lambda i: (0, 0)),
            pl.BlockSpec((1, e), lambda i: (0, 0)),
            pl.BlockSpec((A_WIDTH, A_WIDTH), lambda i: (0, 0)),
        ],
        out_specs=pl.BlockSpec((tm, e), lambda i: (i, 0)),
        out_shape=jax.ShapeDtypeStruct((n, e), F32),
        compiler_params=_params("parallel"),
        name="proj_a",
    )(x, g, w, colgain, blockdiag)


def _proj_b_kernel(x_ref, g_ref, w_ref, q_ref, kv_ref, kvb_ref):
    h = _rms_bf16(x_ref[...], g_ref[...])
    q_ref[...] = (_dot(h, w_ref[:, :B_WIDTH]) * (1.0 / math.sqrt(HEAD_DIM))).astype(BF16)
    for c in range(2):
        cols = slice(c * B_WIDTH, (c + 1) * B_WIDTH)
        y = _dot(h, w_ref[:, B_WIDTH + c * B_WIDTH:B_WIDTH + (c + 1) * B_WIDTH])
        kv_ref[:, cols] = y
        kvb_ref[:, cols] = y.astype(BF16)


def _proj_b(x, g, w, tm):
    n, d = x.shape
    return pl.pallas_call(
        _proj_b_kernel,
        grid=(n // tm,),
        in_specs=[
            pl.BlockSpec((tm, d), lambda i: (i, 0)),
            pl.BlockSpec((1, d), lambda i: (0, 0)),
            pl.BlockSpec((d, 3 * B_WIDTH), lambda i: (0, 0)),
        ],
        out_specs=[
            pl.BlockSpec((tm, B_WIDTH), lambda i: (i, 0)),
            pl.BlockSpec((tm, 2 * B_WIDTH), lambda i: (i, 0)),
            pl.BlockSpec((tm, 2 * B_WIDTH), lambda i: (i, 0)),
        ],
        out_shape=[
            jax.ShapeDtypeStruct((n, B_WIDTH), BF16),
            jax.ShapeDtypeStruct((n, 2 * B_WIDTH), F32),
            jax.ShapeDtypeStruct((n, 2 * B_WIDTH), BF16),
        ],
        compiler_params=_params("parallel"),
        name="proj_b",
    )(x, g, w)


def _combine_wo_kernel(x_ref, o0, o1, o2, l0, l1, l2, w_ref, y_ref):
    m = jnp.maximum(jnp.maximum(l0[...], l1[...]), l2[...])
    e0 = jnp.exp(l0[...] - m)
    e1 = jnp.exp(l1[...] - m)
    e2 = jnp.exp(l2[...] - m)
    o = (e0 * o0[...] + e1 * o1[...] + e2 * o2[...]) / (e0 + e1 + e2)
    y_ref[...] = x_ref[...] + _dot(o.astype(BF16), w_ref[...])


def _combine_wo(x, outs, lses, w, tm):
    n, d = x.shape
    half = pl.BlockSpec((tm, A_WIDTH), lambda i: (i, 0))
    return pl.pallas_call(
        _combine_wo_kernel,
        grid=(n // tm,),
        in_specs=[pl.BlockSpec((tm, d), lambda i: (i, 0))] + [half] * 6
        + [pl.BlockSpec((A_WIDTH, d), lambda i: (0, 0))],
        out_specs=pl.BlockSpec((tm, d), lambda i: (i, 0)),
        out_shape=jax.ShapeDtypeStruct((n, d), F32),
        compiler_params=_params("parallel"),
        name="combine_wo",
    )(x, *outs, *lses, w)


def _wo_kernel(x_ref, o_ref, w_ref, y_ref):
    y_ref[...] = x_ref[...] + _dot(o_ref[...].astype(BF16), w_ref[...])


def _wo(x, o, w, tm):
    n, d = x.shape
    f = o.shape[1]
    return pl.pallas_call(
        _wo_kernel,
        grid=(n // tm,),
        in_specs=[
            pl.BlockSpec((tm, d), lambda i: (i, 0)),
            pl.BlockSpec((tm, f), lambda i: (i, 0)),
            pl.BlockSpec((f, d), lambda i: (0, 0)),
        ],
        out_specs=pl.BlockSpec((tm, d), lambda i: (i, 0)),
        out_shape=jax.ShapeDtypeStruct((n, d), F32),
        compiler_params=_params("parallel"),
        name="wo",
    )(x, o, w)


def _ffn_kernel(x_ref, g_ref, wg_ref, wu_ref, wd_ref, y_ref, h_scr, acc_scr):
    j = pl.program_id(1)

    @pl.when(j == 0)
    def _():
        h_scr[...] = _rms_bf16(x_ref[...], g_ref[...])
        acc_scr[...] = jnp.zeros_like(acc_scr)

    h = h_scr[...]
    a = _dot(h, wg_ref[...])
    b = _dot(h, wu_ref[...])
    s = (a * jax.nn.sigmoid(a) * b).astype(BF16)
    acc_scr[...] += _dot(s, wd_ref[...])

    @pl.when(j == pl.num_programs(1) - 1)
    def _():
        y_ref[...] = x_ref[...] + acc_scr[...]


def _ffn(x, g, wg, wu, wd, tm, tf=256):
    n, d = x.shape
    f = wg.shape[1]
    return pl.pallas_call(
        _ffn_kernel,
        grid=(n // tm, f // tf),
        in_specs=[
            pl.BlockSpec((tm, d), lambda i, j: (i, 0)),
            pl.BlockSpec((1, d), lambda i, j: (0, 0)),
            pl.BlockSpec((d, tf), lambda i, j: (0, j)),
            pl.BlockSpec((d, tf), lambda i, j: (0, j)),
            pl.BlockSpec((tf, d), lambda i, j: (j, 0)),
        ],
        out_specs=pl.BlockSpec((tm, d), lambda i, j: (i, 0)),
        out_shape=jax.ShapeDtypeStruct((n, d), F32),
        scratch_shapes=[pltpu.VMEM((tm, d), BF16), pltpu.VMEM((tm, d), F32)],
        compiler_params=_params("parallel", "arbitrary"),
        name="ffn",
    )(x, g, wg, wu, wd)


def _attn_a_prompt_kernel(q_ref, kp_ref, kc_ref, vp_ref, vc_ref, o_ref, l_ref, *, slopes, dilation):
    u = pl.program_id(2)
    qi = lax.broadcasted_iota(jnp.int32, (QB, QB), 0)
    kk = lax.broadcasted_iota(jnp.int32, (QB, QB), 1)
    steps_p = (qi + A_REACH - kk).astype(F32)
    steps_c = (qi - kk).astype(F32)
    in_p = kk >= qi
    in_c = kk <= qi
    first = jnp.where(u > 0, 0.0, NEG)
    q = q_ref[0].astype(BF16)
    kp = kp_ref[0].astype(BF16)
    kc = kc_ref[0].astype(BF16)
    vp = vp_ref[0].astype(BF16)
    vc = vc_ref[0].astype(BF16)
    outs, lses = [], []
    for h in range(A_HEADS):
        cols = slice(h * HEAD_DIM, (h + 1) * HEAD_DIM)
        slope = float(slopes[h]) * dilation
        qh = q[:, cols]
        sp = jnp.where(in_p, _dot_nt(qh, kp[:, cols]) - slope * steps_p, NEG) + first
        sc = jnp.where(in_c, _dot_nt(qh, kc[:, cols]) - slope * steps_c, NEG)
        m = jnp.maximum(jnp.max(sp, axis=-1, keepdims=True), jnp.max(sc, axis=-1, keepdims=True))
        pp = jnp.exp(sp - m)
        pc = jnp.exp(sc - m)
        l = jnp.sum(pp, axis=-1, keepdims=True) + jnp.sum(pc, axis=-1, keepdims=True)
        o = (_dot(pp.astype(BF16), vp[:, cols]) + _dot(pc.astype(BF16), vc[:, cols])) / l
        outs.append(o)
        lses.append(jnp.broadcast_to(m + jnp.log(l), (QB, HEAD_DIM)))
    o_ref[0] = jnp.concatenate(outs, axis=1)
    l_ref[0] = jnp.concatenate(lses, axis=1)


def _attn_a_prompt(qkv, g, batch, seq):
    _, dilation = A_GROUPS[g]
    n = seq // dilation
    nblk = n // QB
    ncol = 3 * N_GROUPS
    x = qkv.reshape(batch, n, dilation * ncol * A_WIDTH)

    def spec(c, prev):
        if prev:
            return pl.BlockSpec((1, QB, A_WIDTH), lambda b, r, u: (b, jnp.maximum(u - 1, 0), r * ncol + c))
        return pl.BlockSpec((1, QB, A_WIDTH), lambda b, r, u: (b, u, r * ncol + c))

    out_spec = pl.BlockSpec((1, QB, A_WIDTH), lambda b, r, u: (b, u, r))
    o, lse = pl.pallas_call(
        functools.partial(_attn_a_prompt_kernel, slopes=_SLOPES[g], dilation=dilation),
        grid=(batch, dilation, nblk),
        in_specs=[spec(3 * g, False), spec(3 * g + 1, True), spec(3 * g + 1, False),
                  spec(3 * g + 2, True), spec(3 * g + 2, False)],
        out_specs=[out_spec, out_spec],
        out_shape=[jax.ShapeDtypeStruct((batch, n, dilation * A_WIDTH), F32)] * 2,
        compiler_params=_params("parallel", "parallel", "arbitrary"),
        name=f"attn_a_prompt_g{g}",
    )(x, x, x, x, x)
    return o.reshape(batch * seq, A_WIDTH), lse.reshape(batch * seq, A_WIDTH)


def _attn_a_sample_kernel(c_ref, q_ref, new_ref, bias_ref, co_ref, o_ref, l_ref, *, length):
    b = pl.program_id(0)
    nchunk = length // LANES
    lane = lax.broadcasted_iota(jnp.int32, (HEAD_DIM, LANES), 1)
    row8 = lax.broadcasted_iota(jnp.int32, (A_HEADS, 1), 0)

    @pl.when(b == 0)
    def _():
        o_ref[...] = jnp.zeros_like(o_ref)
        l_ref[...] = jnp.zeros_like(l_ref)

    s = bias_ref[...]
    s_new = jnp.zeros((A_HEADS, LANES), F32)
    for h in range(A_HEADS):
        rows = pl.ds(h * HEAD_DIM, HEAD_DIM)
        qh = q_ref[0, rows, :]
        kt = c_ref[0, rows, :]
        sh = jnp.sum(kt * jnp.concatenate([qh] * nchunk, axis=1), axis=0, keepdims=True)
        s = s + jnp.where(row8 == h, sh, 0.0)
        sn = jnp.sum(new_ref[0, rows, :] * qh, axis=0, keepdims=True)
        s_new = s_new + jnp.where(row8 == h, sn, 0.0)
    m = jnp.maximum(jnp.max(s, axis=-1, keepdims=True), s_new)
    p = jnp.exp(s - m[:, 0:1])
    p_new = jnp.exp(s_new - m)
    l = jnp.sum(p, axis=-1, keepdims=True) + p_new
    inv = 1.0 / l
    p = p * inv[:, 0:1]
    p_new = p_new * inv
    lse = m + jnp.log(l)
    lane8 = lax.broadcasted_iota(jnp.int32, (A_HEADS, LANES), 1)
    l_ref[...] = jnp.where(lane8 == b, lse, l_ref[...])

    for h in range(A_HEADS):
        rows = pl.ds(A_WIDTH + h * HEAD_DIM, HEAD_DIM)
        vt = c_ref[0, rows, :]
        acc = jnp.zeros((HEAD_DIM, LANES), F32)
        for c in range(nchunk):
            cl = slice(c * LANES, (c + 1) * LANES)
            acc = acc + vt[:, cl] * p[h:h + 1, cl]
        col = jnp.sum(acc, axis=-1, keepdims=True) + new_ref[0, rows, :] * p_new[h:h + 1, :]
        orow = pl.ds(h * HEAD_DIM, HEAD_DIM)
        o_ref[orow, :] = jnp.where(lane == b, col, o_ref[orow, :])

    for r in range(2 * A_WIDTH // HEAD_DIM):
        rows = pl.ds(r * HEAD_DIM, HEAD_DIM)
        x = pltpu.roll(c_ref[0, rows, :], length - 1, 1)
        if nchunk > 1:
            co_ref[0, rows, pl.ds(0, length - LANES)] = x[:, :length - LANES]
        co_ref[0, rows, pl.ds(length - LANES, LANES)] = jnp.where(
            lane == LANES - 1, new_ref[0, rows, :], x[:, length - LANES:])


def _attn_a_sample(cache_t, q_b, new_b, bias):
    nb, rows, length = cache_t.shape
    return pl.pallas_call(
        functools.partial(_attn_a_sample_kernel, length=length),
        grid=(nb,),
        in_specs=[
            pl.BlockSpec((1, rows, length), lambda b: (b, 0, 0)),
            pl.BlockSpec((1, A_WIDTH, LANES), lambda b: (b, 0, 0)),
            pl.BlockSpec((1, rows, LANES), lambda b: (b, 0, 0)),
            pl.BlockSpec((A_HEADS, length), lambda b: (0, 0)),
        ],
        out_specs=[
            pl.BlockSpec((1, rows, length), lambda b: (b, 0, 0)),
            pl.BlockSpec((A_WIDTH, nb), lambda b: (0, 0)),
            pl.BlockSpec((A_HEADS, nb), lambda b: (0, 0)),
        ],
        out_shape=[
            jax.ShapeDtypeStruct((nb, rows, length), F32),
            jax.ShapeDtypeStruct((A_WIDTH, nb), F32),
            jax.ShapeDtypeStruct((A_HEADS, nb), F32),
        ],
        compiler_params=_params("arbitrary"),
        name=f"attn_a_sample_L{length}",
    )(cache_t, q_b, new_b, bias)


def _sample_bias(g, length):
    _, dilation = A_GROUPS[g]
    i = np.arange(length)
    back = (length - i).astype(np.float32)
    bias = -_SLOPES[g][:, None] * back[None, :]
    return jnp.asarray(np.where((i % dilation == 0)[None, :], bias, np.float32(NEG)).astype(np.float32))


def _stick_tile(qm, kblk, vblk, bias, tt, carry, tri):
    z = _dot_nt(qm, kblk) + bias
    sp = jnp.maximum(z, 0.0) + jnp.log(1.0 + jnp.exp(-jnp.abs(z)))
    if tri is not None:
        sp = jnp.where(tri, sp, 0.0)
    hi, lo = _split_bf16(sp)
    r = _dot(jnp.concatenate([hi, lo], axis=1), tt)
    a = jnp.exp(z - (r[:, :QB] + carry))
    if tri is not None:
        a = jnp.where(tri, a, 0.0)
    return _dot(a.astype(BF16), vblk), carry + r[:, QB:]


def _attn_b_prompt_kernel(bias_ref, q_ref, k_ref, v_ref, tt_ref, o_ref):
    hp = pl.program_id(1)
    i = pl.program_id(2)
    q = q_ref[0]
    tt = tt_ref[...]
    lane = lax.broadcasted_iota(jnp.int32, (QB, LANES), 1)
    row = lax.broadcasted_iota(jnp.int32, (QB, LANES), 0)
    tri = lane < row
    accs = []
    for hh in range(2):
        in_head = (lane >= hh * HEAD_DIM) & (lane < (hh + 1) * HEAD_DIM)
        qm = jnp.where(in_head, q, jnp.zeros_like(q))
        bias = bias_ref[2 * hp + hh]

        def blocks(kb):
            start = pl.multiple_of(kb * QB, QB)
            return k_ref[0, pl.ds(start, QB), :], v_ref[0, pl.ds(start, QB), :]

        kblk, vblk = blocks(i)
        acc0, carry0 = _stick_tile(qm, kblk, vblk, bias, tt, jnp.zeros((QB, QB), F32), tri)

        def body(it, state, qm=qm, bias=bias):
            acc, carry = state
            kblk, vblk = blocks(i - 1 - it)
            pv, carry = _stick_tile(qm, kblk, vblk, bias, tt, carry, None)
            return acc + pv, carry

        acc, _ = lax.fori_loop(0, i, body, (acc0, carry0))
        accs.append(acc)
    o_ref[0] = jnp.where(lane < HEAD_DIM, accs[0], accs[1]).astype(o_ref.dtype)


def _attn_b_prompt(q, kvb, bias, tt, batch, seq):
    q3 = q.reshape(batch, seq, B_WIDTH)
    kv3 = kvb.reshape(batch, seq, 2 * B_WIDTH)
    npair = B_WIDTH // LANES
    o = pl.pallas_call(
        _attn_b_prompt_kernel,
        grid=(batch, npair, seq // QB),
        in_specs=[
            pl.BlockSpec(memory_space=pltpu.SMEM),
            pl.BlockSpec((1, QB, LANES), lambda b, h, i: (b, i, h)),
            pl.BlockSpec((1, seq, LANES), lambda b, h, i: (b, 0, h)),
            pl.BlockSpec((1, seq, LANES), lambda b, h, i: (b, 0, npair + h)),
            pl.BlockSpec((2 * QB, 2 * QB), lambda b, h, i: (0, 0)),
        ],
        out_specs=pl.BlockSpec((1, QB, LANES), lambda b, h, i: (b, i, h)),
        out_shape=jax.ShapeDtypeStruct((batch, seq, B_WIDTH), BF16),
        compiler_params=_params("parallel", "parallel", "arbitrary"),
        name="attn_b_prompt",
    )(bias, q3, kv3, kv3, tt)
    return o.reshape(batch * seq, B_WIDTH)


def _suffix_sum_matrix():
    j = np.arange(2 * QB)[:, None] % QB
    s = np.arange(2 * QB)[None, :]
    return jnp.asarray(np.where(s < QB, j >= s, True).astype(np.float32), dtype=BF16)


def _attn_b_sample_kernel(pt_ref, page_ref, q_ref, bias_ref, tt_ref, o_ref, acc_scr, carry_scr):
    del pt_ref
    b = pl.program_id(0)
    j = pl.program_id(1)
    row16 = lax.broadcasted_iota(jnp.int32, (B_HEADS, 1), 0)

    @pl.when((b == 0) & (j == 0))
    def _():
        o_ref[...] = jnp.zeros_like(o_ref)

    @pl.when(j == 0)
    def _():
        acc_scr[...] = jnp.zeros_like(acc_scr)
        carry_scr[...] = jnp.zeros_like(carry_scr)

    z = bias_ref[...]
    for h in range(B_HEADS):
        rows = pl.ds(h * HEAD_DIM, HEAD_DIM)
        zh = jnp.sum(page_ref[0, rows, :] * q_ref[0, rows, :], axis=0, keepdims=True)
        z = z + jnp.where(row16 == h, zh, 0.0)
    sp = jnp.maximum(z, 0.0) + jnp.log(1.0 + jnp.exp(-jnp.abs(z)))
    hi, lo = _split_bf16(sp)
    r = _dot(jnp.concatenate([hi, lo], axis=1), tt_ref[...])
    a = jnp.exp(z - (r[:, :LANES] + carry_scr[...]))
    carry_scr[...] += r[:, LANES:]
    for h in range(B_HEADS):
        rows = pl.ds(h * HEAD_DIM, HEAD_DIM)
        acc_scr[rows, :] += page_ref[0, pl.ds(B_WIDTH + h * HEAD_DIM, HEAD_DIM), :] * a[h:h + 1, :]

    @pl.when(j == pl.num_programs(1) - 1)
    def _():
        lane = lax.broadcasted_iota(jnp.int32, o_ref.shape, 1)
        col = jnp.sum(acc_scr[...], axis=-1, keepdims=True)
        o_ref[...] = jnp.where(lane == b, col, o_ref[...])


def _attn_b_sample(pages_t, page_table, q_b, bias_b, tt):
    nb, npages = page_table.shape
    rows = pages_t.shape[1]
    grid_spec = pltpu.PrefetchScalarGridSpec(
        num_scalar_prefetch=1,
        grid=(nb, npages),
        in_specs=[
            pl.BlockSpec((1, rows, LANES), lambda b, j, pt: (pt[b, npages - 1 - j], 0, 0)),
            pl.BlockSpec((1, B_WIDTH, LANES), lambda b, j, pt: (b, 0, 0)),
            pl.BlockSpec((B_HEADS, LANES), lambda b, j, pt: (0, 0)),
            pl.BlockSpec((2 * LANES, 2 * LANES), lambda b, j, pt: (0, 0)),
        ],
        out_specs=pl.BlockSpec((B_WIDTH, nb), lambda b, j, pt: (0, 0)),
        scratch_shapes=[pltpu.VMEM((B_WIDTH, LANES), F32), pltpu.VMEM((B_HEADS, LANES), F32)],
    )
    return pl.pallas_call(
        _attn_b_sample_kernel,
        grid_spec=grid_spec,
        out_shape=jax.ShapeDtypeStruct((B_WIDTH, nb), F32),
        compiler_params=_params("arbitrary", "arbitrary"),
        name="attn_b_sample",
    )(page_table, pages_t, q_b, bias_b, tt)


def _lane_repeat(x):
    return jnp.broadcast_to(x[:, :, None], x.shape + (LANES,))


def _layer_a(xp, xs, caches, g_mix, w_qkv, q_gain, k_gain, w_o, batch, seq):
    d = xp.shape[1]
    scale = 1.0 / math.sqrt(HEAD_DIM)
    ones = jnp.ones((A_WIDTH,), F32)
    colgain = jnp.concatenate(
        [jnp.concatenate([jnp.tile(q_gain[g], A_HEADS) * scale, jnp.tile(k_gain[g], A_HEADS), ones])
         for g in range(N_GROUPS)])[None, :]
    seg = np.arange(A_WIDTH) // HEAD_DIM
    blockdiag = jnp.asarray((seg[:, None] == seg[None, :]).astype(np.float32) / HEAD_DIM, dtype=BF16)
    w = w_qkv.astype(BF16)
    wo = w_o.astype(BF16)
    g_mix = g_mix[None, :]

    qkv_p = _proj_a(xp, g_mix, w, colgain, blockdiag, 256)
    outs, lses, states_p = [], [], []
    qkv_p5 = qkv_p.reshape(batch, seq, N_GROUPS, 3, A_HEADS, HEAD_DIM)
    for g, (window, _) in enumerate(A_GROUPS):
        o, lse = _attn_a_prompt(qkv_p, g, batch, seq)
        outs.append(o)
        lses.append(lse)
        keep = min(window, seq)
        states_p.append(jnp.stack([qkv_p5[:, seq - keep:, g, 1], qkv_p5[:, seq - keep:, g, 2]], axis=2))
    xp = _combine_wo(xp, outs, lses, wo, 512)

    nb = xs.shape[0]
    qkv_s = _proj_a(xs, g_mix, w, colgain, blockdiag, nb)
    qkv_s4 = qkv_s.reshape(nb, N_GROUPS, 3, A_WIDTH)
    outs, lses, states_s = [], [], []
    for g in range(N_GROUPS):
        cache = caches[g]
        length = cache.shape[1]
        assert length == A_GROUPS[g][0] and cache.shape[0] == nb and nb == LANES
        cache_t = jnp.transpose(cache, (0, 2, 3, 4, 1)).reshape(nb, 2 * A_WIDTH, length)
        q_b = _lane_repeat(qkv_s4[:, g, 0])
        new_b = _lane_repeat(qkv_s4[:, g, 1:3].reshape(nb, 2 * A_WIDTH))
        new_t, o_t, lse_t = _attn_a_sample(cache_t, q_b, new_b, _sample_bias(g, length))
        states_s.append(jnp.transpose(new_t.reshape(nb, 2, A_HEADS, HEAD_DIM, length), (0, 4, 1, 2, 3)))
        outs.append(o_t.T)
        lses.append(jnp.repeat(lse_t.T, HEAD_DIM, axis=1))
    xs = _combine_wo(xs, outs, lses, wo, nb)
    return xp, xs, tuple(states_p), tuple(states_s)


def _layer_b(xp, xs, pages, page_table, g_mix, w_qkv, logit_bias, w_o, batch, seq):
    w = w_qkv.astype(BF16)
    wo = w_o.astype(BF16)
    g_mix = g_mix[None, :]
    bias = logit_bias.astype(F32)
    tt = _suffix_sum_matrix()

    q, kv, kvb = _proj_b(xp, g_mix, w, 512)
    o = _attn_b_prompt(q, kvb, bias, tt, batch, seq)
    xp = _wo(xp, o, wo, 512)
    state_p = kv.reshape(batch, seq, 2, B_HEADS, HEAD_DIM)

    nb = xs.shape[0]
    assert nb == LANES and pages.shape[1] == LANES
    q_s, kv_s, _ = _proj_b(xs, g_mix, w, nb)
    pages_t = jnp.transpose(pages, (0, 2, 3, 4, 1)).reshape(pages.shape[0], 2 * B_WIDTH, pages.shape[1])
    q_b = _lane_repeat(q_s.astype(F32))
    bias_b = jnp.broadcast_to(bias[:, None], (B_HEADS, LANES))
    o_t = _attn_b_sample(pages_t, page_table, q_b, bias_b, tt)
    xs = _wo(xs, o_t.T, wo, nb)
    state_s = kv_s.reshape(nb, 1, 2, B_HEADS, HEAD_DIM)
    return xp, xs, state_p, state_s


def kernel(x_prompt, x_sample, cache_l0_w128, cache_l0_w512, cache_l0_w2048, cache_l1_kv_pages,
           cache_l2_w128, cache_l2_w512, cache_l2_w2048, cache_l3_kv_pages, page_table,
           norm_mix, norm_ffn, a_w_qkv, a_q_gain, a_k_gain, a_w_o, b_w_qkv, b_logit_bias, b_w_o,
           ffn_w_gate, ffn_w_up, ffn_w_down):
    batch, seq, d = x_prompt.shape
    nb, dec_seq, _ = x_sample.shape
    assert dec_seq == 1
    layer_caches = ((cache_l0_w128, cache_l0_w512, cache_l0_w2048), cache_l1_kv_pages,
                    (cache_l2_w128, cache_l2_w512, cache_l2_w2048), cache_l3_kv_pages)
    xp = x_prompt.reshape(batch * seq, d)
    xs = x_sample.reshape(nb, d)
    states_p, states_s = [], []
    for i in range(norm_mix.shape[0]):
        j = i // 2
        if i % 2 == 0:
            xp, xs, sp, ss = _layer_a(xp, xs, layer_caches[i], norm_mix[i], a_w_qkv[j], a_q_gain[j],
                                      a_k_gain[j], a_w_o[j], batch, seq)
            states_p.extend(sp)
            states_s.extend(ss)
        else:
            xp, xs, sp, ss = _layer_b(xp, xs, layer_caches[i], page_table, norm_mix[i], b_w_qkv[j],
                                      b_logit_bias[j], b_w_o[j], batch, seq)
            states_p.append(sp)
            states_s.append(ss)
        wg = ffn_w_gate[i].astype(BF16)
        wu = ffn_w_up[i].astype(BF16)
        wd = ffn_w_down[i].astype(BF16)
        g_ffn = norm_ffn[i][None, :]
        xp = _ffn(xp, g_ffn, wg, wu, wd, 1024)
        xs = _ffn(xs, g_ffn, wg, wu, wd, nb)
    p0, p1, p2, p3 = states_p[0:3], states_p[3], states_p[4:7], states_p[7]
    s0, s1, s2, s3 = states_s[0:3], states_s[3], states_s[4:7], states_s[7]
    return (xp.reshape(batch, seq, d), xs.reshape(nb, 1, d), *p0, *s0, p1, s1, *p2, *s2, p3, s3)
```

```python
import functools
import math

import numpy as np
import jax
import jax.numpy as jnp
from jax import lax
from jax.experimental import pallas as pl
from jax.experimental.pallas import tpu as pltpu

F32 = jnp.float32
BF16 = jnp.bfloat16

HEAD_DIM = 64
A_HEADS = 8
A_WIDTH = A_HEADS * HEAD_DIM
B_HEADS = 16
B_WIDTH = B_HEADS * HEAD_DIM
A_GROUPS = ((128, 1), (512, 4), (2048, 16))
N_GROUPS = len(A_GROUPS)
A_REACH = 128
QB = 128
LANES = 128
EPS = 1e-6
NEG = -1e30
VMEM_LIMIT = 48 * 1024 * 1024


def _alibi_slopes():
    n = N_GROUPS * A_HEADS
    start = 2.0 ** (-8.0 / n)
    s = np.array([start ** (i + 1) for i in range(n)], dtype=np.float32)
    return s.reshape(N_GROUPS, A_HEADS)


_SLOPES = _alibi_slopes()


def _params(*sem):
    return pltpu.CompilerParams(dimension_semantics=sem, vmem_limit_bytes=VMEM_LIMIT)


def _rms_bf16(x, g):
    ms = jnp.mean(x * x, axis=-1, keepdims=True)
    return (x * lax.rsqrt(ms + EPS) * g).astype(BF16)


def _split_bf16(x):
    hi = x.astype(BF16)
    lo = (x - hi.astype(F32)).astype(BF16)
    return hi, lo


def _dot(a, b):
    return jnp.dot(a, b, preferred_element_type=F32)


def _dot_nt(a, b):
    return lax.dot_general(a, b, (((1,), (1,)), ((), ())), preferred_element_type=F32)


def _proj_a_kernel(x_ref, g_ref, w_ref, cg_ref, bd_ref, o_ref):
    h = _rms_bf16(x_ref[...], g_ref[...])
    for c in range(3 * N_GROUPS):
        cols = slice(c * A_WIDTH, (c + 1) * A_WIDTH)
        y = _dot(h, w_ref[:, cols])
        if c % 3 != 2:
            hi, lo = _split_bf16(y * y)
            ms = _dot(hi, bd_ref[...]) + _dot(lo, bd_ref[...])
            y = y * lax.rsqrt(ms + EPS) * cg_ref[:, cols]
        o_ref[:, cols] = y


def _proj_a(x, g, w, colgain, blockdiag, tm):
    n, d = x.shape
    e = w.shape[1]
    return pl.pallas_call(
        _proj_a_kernel,
        grid=(n // tm,),
        in_specs=[
            pl.BlockSpec((tm, d), lambda i: (i, 0)),
            pl.BlockSpec((1, d), lambda i: (0, 0)),
            pl.BlockSpec((d, e), lambda i: (0, 0)),
            pl.BlockSpec((1, e), lambda i: (0, 0)),
            pl.BlockSpec((A_WIDTH, A_WIDTH), lambda i: (0, 0)),
        ],
        out_specs=pl.BlockSpec((tm, e), lambda i: (i, 0)),
        out_shape=jax.ShapeDtypeStruct((n, e), F32),
        compiler_params=_params("parallel"),
        name="proj_a",
    )(x, g, w, colgain, blockdiag)


def _proj_b_kernel(x_ref, g_ref, w_ref, q_ref, kv_ref, kvb_ref):
    h = _rms_bf16(x_ref[...], g_ref[...])
    q_ref[...] = (_dot(h, w_ref[:, :B_WIDTH]) * (1.0 / math.sqrt(HEAD_DIM))).astype(BF16)
    for c in range(2):
        cols = slice(c * B_WIDTH, (c + 1) * B_WIDTH)
        y = _dot(h, w_ref[:, B_WIDTH + c * B_WIDTH:B_WIDTH + (c + 1) * B_WIDTH])
        kv_ref[:, cols] = y
        kvb_ref[:, cols] = y.astype(BF16)


def _proj_b(x, g, w, tm):
    n, d = x.shape
    return pl.pallas_call(
        _proj_b_kernel,
        grid=(n // tm,),
        in_specs=[
            pl.BlockSpec((tm, d), lambda i: (i, 0)),
            pl.BlockSpec((1, d), lambda i: (0, 0)),
            pl.BlockSpec((d, 3 * B_WIDTH), lambda i: (0, 0)),
        ],
        out_specs=[
            pl.BlockSpec((tm, B_WIDTH), lambda i: (i, 0)),
            pl.BlockSpec((tm, 2 * B_WIDTH), lambda i: (i, 0)),
            pl.BlockSpec((tm, 2 * B_WIDTH), lambda i: (i, 0)),
        ],
        out_shape=[
            jax.ShapeDtypeStruct((n, B_WIDTH), BF16),
            jax.ShapeDtypeStruct((n, 2 * B_WIDTH), F32),
            jax.ShapeDtypeStruct((n, 2 * B_WIDTH), BF16),
        ],
        compiler_params=_params("parallel"),
        name="proj_b",
    )(x, g, w)


def _combine_wo_kernel(x_ref, o0, o1, o2, l0, l1, l2, w_ref, y_ref):
    m = jnp.maximum(jnp.maximum(l0[...], l1[...]), l2[...])
    e0 = jnp.exp(l0[...] - m)
    e1 = jnp.exp(l1[...] - m)
    e2 = jnp.exp(l2[...] - m)
    o = (e0 * o0[...] + e1 * o1[...] + e2 * o2[...]) / (e0 + e1 + e2)
    y_ref[...] = x_ref[...] + _dot(o.astype(BF16), w_ref[...])


def _combine_wo(x, outs, lses, w, tm):
    n, d = x.shape
    half = pl.BlockSpec((tm, A_WIDTH), lambda i: (i, 0))
    return pl.pallas_call(
        _combine_wo_kernel,
        grid=(n // tm,),
        in_specs=[pl.BlockSpec((tm, d), lambda i: (i, 0))] + [half] * 6
        + [pl.BlockSpec((A_WIDTH, d), lambda i: (0, 0))],
        out_specs=pl.BlockSpec((tm, d), lambda i: (i, 0)),
        out_shape=jax.ShapeDtypeStruct((n, d), F32),
        compiler_params=_params("parallel"),
        name="combine_wo",
    )(x, *outs, *lses, w)


def _wo_kernel(x_ref, o_ref, w_ref, y_ref):
    y_ref[...] = x_ref[...] + _dot(o_ref[...].astype(BF16), w_ref[...])


def _wo(x, o, w, tm):
    n, d = x.shape
    f = o.shape[1]
    return pl.pallas_call(
        _wo_kernel,
        grid=(n // tm,),
        in_specs=[
            pl.BlockSpec((tm, d), lambda i: (i, 0)),
            pl.BlockSpec((tm, f), lambda i: (i, 0)),
            pl.BlockSpec((f, d), lambda i: (0, 0)),
        ],
        out_specs=pl.BlockSpec((tm, d), lambda i: (i, 0)),
        out_shape=jax.ShapeDtypeStruct((n, d), F32),
        compiler_params=_params("parallel"),
        name="wo",
    )(x, o, w)


def _ffn_kernel(x_ref, g_ref, wg_ref, wu_ref, wd_ref, y_ref, h_scr, acc_scr):
    j = pl.program_id(1)

    @pl.when(j == 0)
    def _():
        h_scr[...] = _rms_bf16(x_ref[...], g_ref[...])
        acc_scr[...] = jnp.zeros_like(acc_scr)

    h = h_scr[...]
    a = _dot(h, wg_ref[...])
    b = _dot(h, wu_ref[...])
    s = (a * jax.nn.sigmoid(a) * b).astype(BF16)
    acc_scr[...] += _dot(s, wd_ref[...])

    @pl.when(j == pl.num_programs(1) - 1)
    def _():
        y_ref[...] = x_ref[...] + acc_scr[...]


def _ffn(x, g, wg, wu, wd, tm, tf=256):
    n, d = x.shape
    f = wg.shape[1]
    return pl.pallas_call(
        _ffn_kernel,
        grid=(n // tm, f // tf),
        in_specs=[
            pl.BlockSpec((tm, d), lambda i, j: (i, 0)),
            pl.BlockSpec((1, d), lambda i, j: (0, 0)),
            pl.BlockSpec((d, tf), lambda i, j: (0, j)),
            pl.BlockSpec((d, tf), lambda i, j: (0, j)),
            pl.BlockSpec((tf, d), lambda i, j: (j, 0)),
        ],
        out_specs=pl.BlockSpec((tm, d), lambda i, j: (i, 0)),
        out_shape=jax.ShapeDtypeStruct((n, d), F32),
        scratch_shapes=[pltpu.VMEM((tm, d), BF16), pltpu.VMEM((tm, d), F32)],
        compiler_params=_params("parallel", "arbitrary"),
        name="ffn",
    )(x, g, wg, wu, wd)


def _attn_a_prompt_kernel(q_ref, kp_ref, kc_ref, vp_ref, vc_ref, o_ref, l_ref, *, slopes, dilation):
    u = pl.program_id(2)
    qi = lax.broadcasted_iota(jnp.int32, (QB, QB), 0)
    kk = lax.broadcasted_iota(jnp.int32, (QB, QB), 1)
    steps_p = (qi + A_REACH - kk).astype(F32)
    steps_c = (qi - kk).astype(F32)
    in_p = kk >= qi
    in_c = kk <= qi
    first = jnp.where(u > 0, 0.0, NEG)
    q = q_ref[0].astype(BF16)
    kp = kp_ref[0].astype(BF16)
    kc = kc_ref[0].astype(BF16)
    vp = vp_ref[0].astype(BF16)
    vc = vc_ref[0].astype(BF16)
    outs, lses = [], []
    for h in range(A_HEADS):
        cols = slice(h * HEAD_DIM, (h + 1) * HEAD_DIM)
        slope = float(slopes[h]) * dilation
        qh = q[:, cols]
        sp = jnp.where(in_p, _dot_nt(qh, kp[:, cols]) - slope * steps_p, NEG) + first
        sc = jnp.where(in_c, _dot_nt(qh, kc[:, cols]) - slope * steps_c, NEG)
        m = jnp.maximum(jnp.max(sp, axis=-1, keepdims=True), jnp.max(sc, axis=-1, keepdims=True))
        pp = jnp.exp(sp - m)
        pc = jnp.exp(sc - m)
        l = jnp.sum(pp, axis=-1, keepdims=True) + jnp.sum(pc, axis=-1, keepdims=True)
        o = (_dot(pp.astype(BF16), vp[:, cols]) + _dot(pc.astype(BF16), vc[:, cols])) / l
        outs.append(o)
        lses.append(jnp.broadcast_to(m + jnp.log(l), (QB, HEAD_DIM)))
    o_ref[0] = jnp.concatenate(outs, axis=1)
    l_ref[0] = jnp.concatenate(lses, axis=1)


def _attn_a_prompt(qkv, g, batch, seq):
    _, dilation = A_GROUPS[g]
    n = seq // dilation
    nblk = n // QB
    ncol = 3 * N_GROUPS
    x = qkv.reshape(batch, n, dilation * ncol * A_WIDTH)

    def spec(c, prev):
        if prev:
            return pl.BlockSpec((1, QB, A_WIDTH), lambda b, r, u: (b, jnp.maximum(u - 1, 0), r * ncol + c))
        return pl.BlockSpec((1, QB, A_WIDTH), lambda b, r, u: (b, u, r * ncol + c))

    out_spec = pl.BlockSpec((1, QB, A_WIDTH), lambda b, r, u: (b, u, r))
    o, lse = pl.pallas_call(
        functools.partial(_attn_a_prompt_kernel, slopes=_SLOPES[g], dilation=dilation),
        grid=(batch, dilation, nblk),
        in_specs=[spec(3 * g, False), spec(3 * g + 1, True), spec(3 * g + 1, False),
                  spec(3 * g + 2, True), spec(3 * g + 2, False)],
        out_specs=[out_spec, out_spec],
        out_shape=[jax.ShapeDtypeStruct((batch, n, dilation * A_WIDTH), F32)] * 2,
        compiler_params=_params("parallel", "parallel", "arbitrary"),
        name=f"attn_a_prompt_g{g}",
    )(x, x, x, x, x)
    return o.reshape(batch * seq, A_WIDTH), lse.reshape(batch * seq, A_WIDTH)


def _attn_a_sample_kernel(c_ref, q_ref, new_ref, bias_ref, co_ref, o_ref, l_ref, *, length):
    b = pl.program_id(0)
    nchunk = length // LANES
    lane = lax.broadcasted_iota(jnp.int32, (HEAD_DIM, LANES), 1)
    row8 = lax.broadcasted_iota(jnp.int32, (A_HEADS, 1), 0)

    @pl.when(b == 0)
    def _():
        o_ref[...] = jnp.zeros_like(o_ref)
        l_ref[...] = jnp.zeros_like(l_ref)

    s = bias_ref[...]
    s_new = jnp.zeros((A_HEADS, LANES), F32)
    for h in range(A_HEADS):
        rows = pl.ds(h * HEAD_DIM, HEAD_DIM)
        qh = q_ref[0, rows, :]
        kt = c_ref[0, rows, :]
        sh = jnp.sum(kt * jnp.concatenate([qh] * nchunk, axis=1), axis=0, keepdims=True)
        s = s + jnp.where(row8 == h, sh, 0.0)
        sn = jnp.sum(new_ref[0, rows, :] * qh, axis=0, keepdims=True)
        s_new = s_new + jnp.where(row8 == h, sn, 0.0)
    m = jnp.maximum(jnp.max(s, axis=-1, keepdims=True), s_new)
    p = jnp.exp(s - m[:, 0:1])
    p_new = jnp.exp(s_new - m)
    l = jnp.sum(p, axis=-1, keepdims=True) + p_new
    inv = 1.0 / l
    p = p * inv[:, 0:1]
    p_new = p_new * inv
    lse = m + jnp.log(l)
    lane8 = lax.broadcasted_iota(jnp.int32, (A_HEADS, LANES), 1)
    l_ref[...] = jnp.where(lane8 == b, lse, l_ref[...])

    for h in range(A_HEADS):
        rows = pl.ds(A_WIDTH + h * HEAD_DIM, HEAD_DIM)
        vt = c_ref[0, rows, :]
        acc = jnp.zeros((HEAD_DIM, LANES), F32)
        for c in range(nchunk):
            cl = slice(c * LANES, (c + 1) * LANES)
            acc = acc + vt[:, cl] * p[h:h + 1, cl]
        col = jnp.sum(acc, axis=-1, keepdims=True) + new_ref[0, rows, :] * p_new[h:h + 1, :]
        orow = pl.ds(h * HEAD_DIM, HEAD_DIM)
        o_ref[orow, :] = jnp.where(lane == b, col, o_ref[orow, :])

    for r in range(2 * A_WIDTH // HEAD_DIM):
        rows = pl.ds(r * HEAD_DIM, HEAD_DIM)
        x = pltpu.roll(c_ref[0, rows, :], length - 1, 1)
        if nchunk > 1:
            co_ref[0, rows, pl.ds(0, length - LANES)] = x[:, :length - LANES]
        co_ref[0, rows, pl.ds(length - LANES, LANES)] = jnp.where(
            lane == LANES - 1, new_ref[0, rows, :], x[:, length - LANES:])


def _attn_a_sample(cache_t, q_b, new_b, bias):
    nb, rows, length = cache_t.shape
    return pl.pallas_call(
        functools.partial(_attn_a_sample_kernel, length=length),
        grid=(nb,),
        in_specs=[
            pl.BlockSpec((1, rows, length), lambda b: (b, 0, 0)),
            pl.BlockSpec((1, A_WIDTH, LANES), lambda b: (b, 0, 0)),
            pl.BlockSpec((1, rows, LANES), lambda b: (b, 0, 0)),
            pl.BlockSpec((A_HEADS, length), lambda b: (0, 0)),
        ],
        out_specs=[
            pl.BlockSpec((1, rows, length), lambda b: (b, 0, 0)),
            pl.BlockSpec((A_WIDTH, nb), lambda b: (0, 0)),
            pl.BlockSpec((A_HEADS, nb), lambda b: (0, 0)),
        ],
        out_shape=[
            jax.ShapeDtypeStruct((nb, rows, length), F32),
            jax.ShapeDtypeStruct((A_WIDTH, nb), F32),
            jax.ShapeDtypeStruct((A_HEADS, nb), F32),
        ],
        compiler_params=_params("arbitrary"),
        name=f"attn_a_sample_L{length}",
    )(cache_t, q_b, new_b, bias)


def _sample_bias(g, length):
    _, dilation = A_GROUPS[g]
    i = np.arange(length)
    back = (length - i).astype(np.float32)
    bias = -_SLOPES[g][:, None] * back[None, :]
    return jnp.asarray(np.where((i % dilation == 0)[None, :], bias, np.float32(NEG)).astype(np.float32))


def _softplus(z):
    neg_abs = pltpu.bitcast(pltpu.bitcast(z, jnp.uint32) | jnp.uint32(0x80000000), F32)
    return jnp.maximum(z, 0.0) + jnp.log(1.0 + jnp.exp(neg_abs))


def _stick_tile(qm, kcat, vcat, bias, tt, carry, tris):
    z = _dot_nt(qm, kcat) + bias
    sp = _softplus(z)
    a_blocks = [None] * len(tris)
    for n in range(len(tris) - 1, -1, -1):
        cols = slice(n * QB, (n + 1) * QB)
        sp_n = sp[:, cols] if tris[n] is None else jnp.where(tris[n], sp[:, cols], 0.0)
        hi, lo = _split_bf16(sp_n)
        r = _dot(jnp.concatenate([hi, lo], axis=1), tt) + jnp.concatenate([carry, carry], axis=1)
        a = jnp.exp(z[:, cols] - r[:, :QB])
        if tris[n] is not None:
            a = jnp.where(tris[n], a, 0.0)
        a_blocks[n] = a.astype(BF16)
        carry = r[:, QB:]
    return _dot(jnp.concatenate(a_blocks, axis=1), vcat), carry


def _attn_b_prompt_kernel(bias_ref, q_ref, k_ref, v_ref, tt_ref, o_ref, acc_scr, carry_scr, *, tq, unroll):
    hp = pl.program_id(1)
    i = pl.program_id(2)
    nsub = tq // QB
    q = q_ref[0]
    tt = tt_ref[...]
    lane = lax.broadcasted_iota(jnp.int32, (tq, LANES), 1)
    row = lax.broadcasted_iota(jnp.int32, (tq, LANES), 0)
    zero = jnp.zeros_like(q)
    qms = (jnp.where(lane < HEAD_DIM, q, zero), jnp.where(lane >= HEAD_DIM, q, zero))
    biases = (bias_ref[2 * hp], bias_ref[2 * hp + 1])
    acc_scr[...] = jnp.zeros_like(acc_scr)
    carry_scr[...] = jnp.zeros_like(carry_scr)

    def step(kb0, tris):
        rows = pl.ds(pl.multiple_of((kb0 - len(tris) + 1) * QB, QB), len(tris) * QB)
        kcat = k_ref[0, rows, :]
        vcat = v_ref[0, rows, :]
        for hh in range(2):
            pv, carry = _stick_tile(qms[hh], kcat, vcat, biases[hh], tt, carry_scr[hh], tris)
            acc_scr[hh] += pv
            carry_scr[hh] = carry

    step(i * nsub + nsub - 1, [(lane + d * QB) < row for d in range(nsub)])

    def body(it, c):
        step(i * nsub - 1 - it * unroll, [None] * unroll)
        return c

    lax.fori_loop(0, i * (nsub // unroll), body, 0)
    o_ref[0] = jnp.where(lane < HEAD_DIM, acc_scr[0], acc_scr[1]).astype(o_ref.dtype)


def _attn_b_prompt(q, kvb, bias, tt, batch, seq, tq=512, unroll=4):
    q3 = q.reshape(batch, seq, B_WIDTH)
    kv3 = kvb.reshape(batch, seq, 2 * B_WIDTH)
    npair = B_WIDTH // LANES
    o = pl.pallas_call(
        functools.partial(_attn_b_prompt_kernel, tq=tq, unroll=unroll),
        grid=(batch, npair, seq // tq),
        in_specs=[
            pl.BlockSpec(memory_space=pltpu.SMEM),
            pl.BlockSpec((1, tq, LANES), lambda b, h, i: (b, i, h)),
            pl.BlockSpec((1, seq, LANES), lambda b, h, i: (b, 0, h)),
            pl.BlockSpec((1, seq, LANES), lambda b, h, i: (b, 0, npair + h)),
            pl.BlockSpec((2 * QB, 2 * QB), lambda b, h, i: (0, 0)),
        ],
        out_specs=pl.BlockSpec((1, tq, LANES), lambda b, h, i: (b, i, h)),
        out_shape=jax.ShapeDtypeStruct((batch, seq, B_WIDTH), BF16),
        scratch_shapes=[pltpu.VMEM((2, tq, LANES), F32), pltpu.VMEM((2, tq, LANES), F32)],
        compiler_params=_params("parallel", "parallel", "arbitrary"),
        name="attn_b_prompt",
    )(bias, q3, kv3, kv3, tt)
    return o.reshape(batch * seq, B_WIDTH)


def _suffix_sum_matrix():
    j = np.arange(2 * QB)[:, None] % QB
    s = np.arange(2 * QB)[None, :]
    return jnp.asarray(np.where(s < QB, j >= s, True).astype(np.float32), dtype=BF16)


def _attn_b_sample_kernel(pt_ref, page_ref, q_ref, bias_ref, tt_ref, o_ref, acc_scr, carry_scr):
    del pt_ref
    b = pl.program_id(0)
    j = pl.program_id(1)
    row16 = lax.broadcasted_iota(jnp.int32, (B_HEADS, 1), 0)

    @pl.when((b == 0) & (j == 0))
    def _():
        o_ref[...] = jnp.zeros_like(o_ref)

    @pl.when(j == 0)
    def _():
        acc_scr[...] = jnp.zeros_like(acc_scr)
        carry_scr[...] = jnp.zeros_like(carry_scr)

    z = bias_ref[...]
    for h in range(B_HEADS):
        rows = pl.ds(h * HEAD_DIM, HEAD_DIM)
        zh = jnp.sum(page_ref[0, rows, :] * q_ref[0, rows, :], axis=0, keepdims=True)
        z = z + jnp.where(row16 == h, zh, 0.0)
    sp = jnp.maximum(z, 0.0) + jnp.log(1.0 + jnp.exp(-jnp.abs(z)))
    hi, lo = _split_bf16(sp)
    r = _dot(jnp.concatenate([hi, lo], axis=1), tt_ref[...])
    a = jnp.exp(z - (r[:, :LANES] + carry_scr[...]))
    carry_scr[...] += r[:, LANES:]
    for h in range(B_HEADS):
        rows = pl.ds(h * HEAD_DIM, HEAD_DIM)
        acc_scr[rows, :] += page_ref[0, pl.ds(B_WIDTH + h * HEAD_DIM, HEAD_DIM), :] * a[h:h + 1, :]

    @pl.when(j == pl.num_programs(1) - 1)
    def _():
        lane = lax.broadcasted_iota(jnp.int32, o_ref.shape, 1)
        col = jnp.sum(acc_scr[...], axis=-1, keepdims=True)
        o_ref[...] = jnp.where(lane == b, col, o_ref[...])


def _attn_b_sample(pages_t, page_table, q_b, bias_b, tt):
    nb, npages = page_table.shape
    rows = pages_t.shape[1]
    grid_spec = pltpu.PrefetchScalarGridSpec(
        num_scalar_prefetch=1,
        grid=(nb, npages),
        in_specs=[
            pl.BlockSpec((1, rows, LANES), lambda b, j, pt: (pt[b, npages - 1 - j], 0, 0)),
            pl.BlockSpec((1, B_WIDTH, LANES), lambda b, j, pt: (b, 0, 0)),
            pl.BlockSpec((B_HEADS, LANES), lambda b, j, pt: (0, 0)),
            pl.BlockSpec((2 * LANES, 2 * LANES), lambda b, j, pt: (0, 0)),
        ],
        out_specs=pl.BlockSpec((B_WIDTH, nb), lambda b, j, pt: (0, 0)),
        scratch_shapes=[pltpu.VMEM((B_WIDTH, LANES), F32), pltpu.VMEM((B_HEADS, LANES), F32)],
    )
    return pl.pallas_call(
        _attn_b_sample_kernel,
        grid_spec=grid_spec,
        out_shape=jax.ShapeDtypeStruct((B_WIDTH, nb), F32),
        compiler_params=_params("arbitrary", "arbitrary"),
        name="attn_b_sample",
    )(page_table, pages_t, q_b, bias_b, tt)


def _lane_repeat(x):
    return jnp.broadcast_to(x[:, :, None], x.shape + (LANES,))


def _layer_a(xp, xs, caches, g_mix, w_qkv, q_gain, k_gain, w_o, batch, seq):
    d = xp.shape[1]
    scale = 1.0 / math.sqrt(HEAD_DIM)
    ones = jnp.ones((A_WIDTH,), F32)
    colgain = jnp.concatenate(
        [jnp.concatenate([jnp.tile(q_gain[g], A_HEADS) * scale, jnp.tile(k_gain[g], A_HEADS), ones])
         for g in range(N_GROUPS)])[None, :]
    seg = np.arange(A_WIDTH) // HEAD_DIM
    blockdiag = jnp.asarray((seg[:, None] == seg[None, :]).astype(np.float32) / HEAD_DIM, dtype=BF16)
    w = w_qkv.astype(BF16)
    wo = w_o.astype(BF16)
    g_mix = g_mix[None, :]

    qkv_p = _proj_a(xp, g_mix, w, colgain, blockdiag, 256)
    outs, lses, states_p = [], [], []
    qkv_p5 = qkv_p.reshape(batch, seq, N_GROUPS, 3, A_HEADS, HEAD_DIM)
    for g, (window, _) in enumerate(A_GROUPS):
        o, lse = _attn_a_prompt(qkv_p, g, batch, seq)
        outs.append(o)
        lses.append(lse)
        keep = min(window, seq)
        states_p.append(jnp.stack([qkv_p5[:, seq - keep:, g, 1], qkv_p5[:, seq - keep:, g, 2]], axis=2))
    xp = _combine_wo(xp, outs, lses, wo, 512)

    nb = xs.shape[0]
    qkv_s = _proj_a(xs, g_mix, w, colgain, blockdiag, nb)
    qkv_s4 = qkv_s.reshape(nb, N_GROUPS, 3, A_WIDTH)
    outs, lses, states_s = [], [], []
    for g in range(N_GROUPS):
        cache = caches[g]
        length = cache.shape[1]
        assert length == A_GROUPS[g][0] and cache.shape[0] == nb and nb == LANES
        cache_t = jnp.transpose(cache, (0, 2, 3, 4, 1)).reshape(nb, 2 * A_WIDTH, length)
        q_b = _lane_repeat(qkv_s4[:, g, 0])
        new_b = _lane_repeat(qkv_s4[:, g, 1:3].reshape(nb, 2 * A_WIDTH))
        new_t, o_t, lse_t = _attn_a_sample(cache_t, q_b, new_b, _sample_bias(g, length))
        states_s.append(jnp.transpose(new_t.reshape(nb, 2, A_HEADS, HEAD_DIM, length), (0, 4, 1, 2, 3)))
        outs.append(o_t.T)
        lses.append(jnp.repeat(lse_t.T, HEAD_DIM, axis=1))
    xs = _combine_wo(xs, outs, lses, wo, nb)
    return xp, xs, tuple(states_p), tuple(states_s)


def _layer_b(xp, xs, pages, page_table, g_mix, w_qkv, logit_bias, w_o, batch, seq):
    w = w_qkv.astype(BF16)
    wo = w_o.astype(BF16)
    g_mix = g_mix[None, :]
    bias = logit_bias.astype(F32)
    tt = _suffix_sum_matrix()

    q, kv, kvb = _proj_b(xp, g_mix, w, 512)
    o = _attn_b_prompt(q, kvb, bias, tt, batch, seq)
    xp = _wo(xp, o, wo, 512)
    state_p = kv.reshape(batch, seq, 2, B_HEADS, HEAD_DIM)

    nb = xs.shape[0]
    assert nb == LANES and pages.shape[1] == LANES
    q_s, kv_s, _ = _proj_b(xs, g_mix, w, nb)
    pages_t = jnp.transpose(pages, (0, 2, 3, 4, 1)).reshape(pages.shape[0], 2 * B_WIDTH, pages.shape[1])
    q_b = _lane_repeat(q_s.astype(F32))
    bias_b = jnp.broadcast_to(bias[:, None], (B_HEADS, LANES))
    o_t = _attn_b_sample(pages_t, page_table, q_b, bias_b, tt)
    xs = _wo(xs, o_t.T, wo, nb)
    state_s = kv_s.reshape(nb, 1, 2, B_HEADS, HEAD_DIM)
    return xp, xs, state_p, state_s


def kernel(x_prompt, x_sample, cache_l0_w128, cache_l0_w512, cache_l0_w2048, cache_l1_kv_pages,
           cache_l2_w128, cache_l2_w512, cache_l2_w2048, cache_l3_kv_pages, page_table,
           norm_mix, norm_ffn, a_w_qkv, a_q_gain, a_k_gain, a_w_o, b_w_qkv, b_logit_bias, b_w_o,
           ffn_w_gate, ffn_w_up, ffn_w_down):
    batch, seq, d = x_prompt.shape
    nb, dec_seq, _ = x_sample.shape
    assert dec_seq == 1
    layer_caches = ((cache_l0_w128, cache_l0_w512, cache_l0_w2048), cache_l1_kv_pages,
                    (cache_l2_w128, cache_l2_w512, cache_l2_w2048), cache_l3_kv_pages)
    xp = x_prompt.reshape(batch * seq, d)
    xs = x_sample.reshape(nb, d)
    states_p, states_s = [], []
    for i in range(norm_mix.shape[0]):
        j = i // 2
        if i % 2 == 0:
            xp, xs, sp, ss = _layer_a(xp, xs, layer_caches[i], norm_mix[i], a_w_qkv[j], a_q_gain[j],
                                      a_k_gain[j], a_w_o[j], batch, seq)
            states_p.extend(sp)
            states_s.extend(ss)
        else:
            xp, xs, sp, ss = _layer_b(xp, xs, layer_caches[i], page_table, norm_mix[i], b_w_qkv[j],
                                      b_logit_bias[j], b_w_o[j], batch, seq)
            states_p.append(sp)
            states_s.append(ss)
        wg = ffn_w_gate[i].astype(BF16)
        wu = ffn_w_up[i].astype(BF16)
        wd = ffn_w_down[i].astype(BF16)
        g_ffn = norm_ffn[i][None, :]
        xp = _ffn(xp, g_ffn, wg, wu, wd, 1024)
        xs = _ffn(xs, g_ffn, wg, wu, wd, nb)
    p0, p1, p2, p3 = states_p[0:3], states_p[3], states_p[4:7], states_p[7]
    s0, s1, s2, s3 = states_s[0:3], states_s[3], states_s[4:7], states_s[7]
    return (xp.reshape(batch, seq, d), xs.reshape(nb, 1, d), *p0, *s0, p1, s1, *p2, *s2, p3, s3)
```
